```python
import jax, jax.numpy as jnp
from jax import lax
import numpy as np

D_MODEL = 1024
BATCH = 4
SEQ = 4096
DEPTH = 1
DEC_BATCH = 16
DEC_SEQ = 4096
PAST_LEN = 128

CHUNK = 128
A_WIDTH = D_MODEL // 2
A_HEADS = 4
A_HEAD_DIM = A_WIDTH // A_HEADS
B_WIDTH = D_MODEL - A_WIDTH
B_GROUPS = 4
B_GROUP_DIM = B_WIDTH // B_GROUPS
MIX_WIDTH = A_WIDTH + B_WIDTH
CONV_WIDTH = 31
CONV_PAD = CONV_WIDTH // 2
D_FF = -(-8 * D_MODEL // (3 * 256)) * 256
EPS = 1e-6

kernel_name = "hybrid_gmlp_conformer_encoder"


def rms_norm(x, g):
    xf = x.astype(jnp.float32)
    y = xf * lax.rsqrt(jnp.mean(xf * xf, axis=-1, keepdims=True) + EPS)
    return (y * g.astype(jnp.float32)).astype(x.dtype)


def group_layer_norm(x, g, b, n_groups):
    shp = x.shape
    xf = x.astype(jnp.float32).reshape(shp[:-1] + (n_groups, shp[-1] // n_groups))
    mu = jnp.mean(xf, axis=-1, keepdims=True)
    var = jnp.mean(jnp.square(xf - mu), axis=-1, keepdims=True)
    y = ((xf - mu) * lax.rsqrt(var + EPS)).reshape(shp)
    return (y * g.astype(jnp.float32) + b.astype(jnp.float32)).astype(x.dtype)


def mixer_a(u, v, ln_g, ln_b, sp_w, sp_b):
    bsz, s, _ = v.shape
    u = jax.nn.gelu(u)
    v = group_layer_norm(jax.nn.gelu(v), ln_g, ln_b, A_HEADS)
    vc = v.reshape(bsz, s // CHUNK, CHUNK, A_HEADS, A_HEAD_DIM)
    sv = jnp.einsum('hqp,bcphe->bcqhe', sp_w.astype(v.dtype), vc)
    sv = sv + jnp.transpose(sp_b).astype(v.dtype)[None, None, :, :, None]
    return u * sv.reshape(bsz, s, A_WIDTH)


def mixer_b(val, gate, conv_w, conv_b, ln_g, ln_b):
    z = val * jax.nn.sigmoid(gate)
    z = lax.conv_general_dilated(
        z, conv_w.astype(z.dtype)[:, None, :], window_strides=(1,),
        padding=[(CONV_PAD, CONV_PAD)], dimension_numbers=('NWC', 'WIO', 'NWC'),
        feature_group_count=B_WIDTH) + conv_b.astype(z.dtype)
    z = group_layer_norm(z, ln_g, ln_b, B_GROUPS)
    return jax.nn.silu(z)


def encoder_layer(x, mix_pre_g, w_in, a_ln_g, a_ln_b, a_sp_w, a_sp_b,
                  b_conv_w, b_conv_b, b_ln_g, b_ln_b, grp_g, w_out, mix_post_g,
                  ffn_pre_g, w_gate_up, w_down, ffn_post_g):
    dt = x.dtype
    hn = rms_norm(x, mix_pre_g)
    proj = jnp.einsum('bsd,df->bsf', hn, w_in.astype(dt))
    u_a = proj[..., :A_WIDTH]
    v_a = proj[..., A_WIDTH:2 * A_WIDTH]
    val_b = proj[..., 2 * A_WIDTH:2 * A_WIDTH + B_WIDTH]
    gate_b = proj[..., 2 * A_WIDTH + B_WIDTH:]
    out_a = mixer_a(u_a, v_a, a_ln_g, a_ln_b, a_sp_w, a_sp_b)
    out_b = mixer_b(val_b, gate_b, b_conv_w, b_conv_b, b_ln_g, b_ln_b)
    mixed = jnp.concatenate([rms_norm(out_a, grp_g[:A_WIDTH]),
                             rms_norm(out_b, grp_g[A_WIDTH:])], axis=-1)
    mix_out = jnp.einsum('bsf,fd->bsd', mixed, w_out.astype(dt))
    h = x + rms_norm(mix_out, mix_post_g)
    hn2 = rms_norm(h, ffn_pre_g)
    gu = jnp.einsum('bsd,df->bsf', hn2, w_gate_up.astype(dt))
    act = jax.nn.silu(gu[..., :D_FF]) * gu[..., D_FF:]
    ffn_out = jnp.einsum('bsf,fd->bsd', act, w_down.astype(dt))
    return h + rms_norm(ffn_out, ffn_post_g)


def setup_inputs(seed: int = 0) -> dict:
    key = jax.random.key(seed)
    ks = jax.random.split(key, 20)
    f32 = jnp.float32
    def nrm(k, shape, scale):
        return jax.random.normal(k, shape, f32) * scale
    def gain(k, shape):
        return 1.0 + 0.02 * jax.random.normal(k, shape, f32)
    return {
        "x_prompt": jax.random.normal(ks[0], (BATCH, SEQ, D_MODEL), f32),
        "x_sample": jax.random.normal(ks[1], (DEC_BATCH, DEC_SEQ, D_MODEL), f32),
        "mix_pre_g": gain(ks[2], (DEPTH, D_MODEL)),
        "w_in": nrm(ks[3], (DEPTH, D_MODEL, 2 * MIX_WIDTH), D_MODEL ** -0.5),
        "a_ln_g": gain(ks[4], (DEPTH, A_WIDTH)),
        "a_ln_b": nrm(ks[5], (DEPTH, A_WIDTH), 0.02),
        "a_sp_w": nrm(ks[6], (DEPTH, A_HEADS, CHUNK, CHUNK), CHUNK ** -0.5),
        "a_sp_b": nrm(ks[7], (DEPTH, A_HEADS, CHUNK), 0.02),
        "b_conv_w": nrm(ks[8], (DEPTH, CONV_WIDTH, B_WIDTH), CONV_WIDTH ** -0.5),
        "b_conv_b": nrm(ks[9], (DEPTH, B_WIDTH), 0.02),
        "b_ln_g": gain(ks[10], (DEPTH, B_WIDTH)),
        "b_ln_b": nrm(ks[11], (DEPTH, B_WIDTH), 0.02),
        "grp_g": gain(ks[12], (DEPTH, MIX_WIDTH)),
        "w_out": nrm(ks[13], (DEPTH, MIX_WIDTH, D_MODEL), MIX_WIDTH ** -0.5),
        "mix_post_g": gain(ks[14], (DEPTH, D_MODEL)),
        "ffn_pre_g": gain(ks[15], (DEPTH, D_MODEL)),
        "w_gate_up": nrm(ks[16], (DEPTH, D_MODEL, 2 * D_FF), D_MODEL ** -0.5),
        "w_down": nrm(ks[17], (DEPTH, D_FF, D_MODEL), D_FF ** -0.5),
        "ffn_post_g": gain(ks[18], (DEPTH, D_MODEL)),
    }


def reference(x_prompt, x_sample, mix_pre_g, w_in, a_ln_g, a_ln_b, a_sp_w, a_sp_b,
              b_conv_w, b_conv_b, b_ln_g, b_ln_b, grp_g, w_out, mix_post_g,
              ffn_pre_g, w_gate_up, w_down, ffn_post_g):
    y_prompt = x_prompt
    y_sample = x_sample
    for l in range(DEPTH):
        params = (mix_pre_g[l], w_in[l], a_ln_g[l], a_ln_b[l], a_sp_w[l], a_sp_b[l],
                  b_conv_w[l], b_conv_b[l], b_ln_g[l], b_ln_b[l], grp_g[l], w_out[l],
                  mix_post_g[l], ffn_pre_g[l], w_gate_up[l], w_down[l], ffn_post_g[l])
        y_prompt = encoder_layer(y_prompt, *params)
        y_sample = encoder_layer(y_sample, *params)
    return (y_prompt, y_sample)
```

```python
import functools

import jax
import jax.numpy as jnp
from jax import lax
from jax.experimental import pallas as pl
from jax.experimental.pallas import tpu as pltpu

CHUNK = 128
HEADS = 4
GROUP = 128
CONV_WIDTH = 31
CONV_PAD = CONV_WIDTH // 2
HALO = 16
EPS = 1e-6

SEQ_TILE = 512
FF_CHUNK = 256
VMEM_LIMIT_BYTES = 56 * 1024 * 1024


def _rms(x, g):
    return x * lax.rsqrt(jnp.mean(x * x, axis=-1, keepdims=True) + EPS) * g


def _layer_norm(x, g, b):
    mu = jnp.mean(x, axis=-1, keepdims=True)
    xc = x - mu
    var = jnp.mean(xc * xc, axis=-1, keepdims=True)
    return xc * lax.rsqrt(var + EPS) * g + b


def _mixer_kernel(x_ref, xprev_ref, xnext_ref, pre_g_ref, w_in_ref, a_ln_g_ref, a_ln_b_ref,
                  sp_w_ref, sp_b_ref, conv_w_ref, conv_b_ref, b_ln_g_ref, b_ln_b_ref,
                  grp_g_ref, w_out_ref, post_g_ref, h_ref,
                  hn_ref, proj_ref, z_ref, mixed_ref):
    ts = x_ref.shape[0]
    aw = HEADS * GROUP
    s = pl.program_id(1)
    last = pl.num_programs(1) - 1
    pre_g = pre_g_ref[...]

    hn_ref[pl.ds(0, HALO), :] = _rms(xprev_ref[...], pre_g).astype(jnp.bfloat16)
    hn_ref[pl.ds(HALO, ts), :] = _rms(x_ref[...], pre_g).astype(jnp.bfloat16)
    hn_ref[pl.ds(HALO + ts, HALO), :] = _rms(xnext_ref[...], pre_g).astype(jnp.bfloat16)

    proj_ref[...] = jnp.dot(hn_ref[...], w_in_ref[...], preferred_element_type=jnp.float32)

    z_ref[...] = proj_ref[:, pl.ds(2 * aw, aw)] * jax.nn.sigmoid(proj_ref[:, pl.ds(3 * aw, aw)])

    @pl.when(s == 0)
    def _():
        z_ref[pl.ds(0, HALO), :] = jnp.zeros((HALO, aw), jnp.float32)

    @pl.when(s == last)
    def _():
        z_ref[pl.ds(HALO + ts, HALO), :] = jnp.zeros((HALO, aw), jnp.float32)

    grp_g = grp_g_ref[...]
    for c in range(ts // CHUNK):
        r0 = c * CHUNK
        rows = pl.ds(HALO + r0, CHUNK)

        u = jax.nn.gelu(proj_ref[rows, pl.ds(0, aw)])
        v = jax.nn.gelu(proj_ref[rows, pl.ds(aw, aw)])
        out_a = []
        for h in range(HEADS):
            cols = slice(h * GROUP, (h + 1) * GROUP)
            vn = _layer_norm(v[:, cols], a_ln_g_ref[:, cols], a_ln_b_ref[:, cols])
            sv = jnp.dot(sp_w_ref[h], vn.astype(jnp.bfloat16), preferred_element_type=jnp.float32)
            out_a.append(u[:, cols] * (sv + sp_b_ref[:, cols]))
        out_a = jnp.concatenate(out_a, axis=-1)
        mixed_ref[pl.ds(r0, CHUNK), pl.ds(0, aw)] = _rms(out_a, grp_g[:, :aw]).astype(jnp.bfloat16)

        out_b = []
        for h in range(HEADS):
            cols = pl.ds(h * GROUP, GROUP)
            acc = jnp.zeros((CHUNK, GROUP), jnp.float32)
            for k in range(CONV_WIDTH):
                acc = acc + z_ref[pl.ds(r0 + HALO - CONV_PAD + k, CHUNK), cols] * conv_w_ref[pl.ds(k, 1), cols]
            acc = acc + conv_b_ref[:, cols]
            out_b.append(jax.nn.silu(_layer_norm(acc, b_ln_g_ref[:, cols], b_ln_b_ref[:, cols])))
        out_b = jnp.concatenate(out_b, axis=-1)
        mixed_ref[pl.ds(r0, CHUNK), pl.ds(aw, aw)] = _rms(out_b, grp_g[:, aw:]).astype(jnp.bfloat16)

    mix_out = jnp.dot(mixed_ref[...], w_out_ref[...], preferred_element_type=jnp.float32)
    h_ref[...] = x_ref[...] + _rms(mix_out, post_g_ref[...])


def _ffn_kernel(h_ref, pre_g_ref, w_gu_ref, w_down_ref, post_g_ref, y_ref, hn_ref, acc_ref):
    d_ff = w_down_ref.shape[0]
    hn_ref[...] = _rms(h_ref[...], pre_g_ref[...]).astype(jnp.bfloat16)
    acc_ref[...] = jnp.zeros_like(acc_ref)

    def step(j, carry):
        c0 = pl.multiple_of(j * FF_CHUNK, FF_CHUNK)
        hn = hn_ref[...]
        gate = jnp.dot(hn, w_gu_ref[:, pl.ds(c0, FF_CHUNK)], preferred_element_type=jnp.float32)
        up = jnp.dot(hn, w_gu_ref[:, pl.ds(d_ff + c0, FF_CHUNK)], preferred_element_type=jnp.float32)
        act = (jax.nn.silu(gate) * up).astype(jnp.bfloat16)
        acc_ref[...] += jnp.dot(act, w_down_ref[pl.ds(c0, FF_CHUNK), :], preferred_element_type=jnp.float32)
        return carry

    lax.fori_loop(0, d_ff // FF_CHUNK, step, 0)
    y_ref[...] = h_ref[...] + _rms(acc_ref[...], post_g_ref[...])


def _resident(shape):
    return pl.BlockSpec(shape, lambda b, s: (0,) * len(shape), pipeline_mode=pl.Buffered(1))


def _mixer_call(x, pre_g, w_in, a_ln_g, a_ln_b, sp_w, sp_b, conv_w, conv_b, b_ln_g, b_ln_b,
                grp_g, w_out, post_g):
    bsz, seq, d = x.shape
    ts = SEQ_TILE
    assert seq % ts == 0 and ts % CHUNK == 0 and ts % HALO == 0
    halo_blocks_per_tile = ts // HALO
    n_halo_blocks = seq // HALO
    aw = HEADS * GROUP
    tile = pl.BlockSpec((None, ts, d), lambda b, s: (b, s, 0))
    prev_halo = pl.BlockSpec(
        (None, HALO, d), lambda b, s: (b, jnp.maximum(s * halo_blocks_per_tile - 1, 0), 0))
    next_halo = pl.BlockSpec(
        (None, HALO, d),
        lambda b, s: (b, jnp.minimum((s + 1) * halo_blocks_per_tile, n_halo_blocks - 1), 0))
    params = (pre_g, w_in, a_ln_g, a_ln_b, sp_w, sp_b, conv_w, conv_b, b_ln_g, b_ln_b,
              grp_g, w_out, post_g)
    return pl.pallas_call(
        _mixer_kernel,
        grid=(bsz, seq // ts),
        in_specs=[tile, prev_halo, next_halo] + [_resident(p.shape) for p in params],
        out_specs=tile,
        out_shape=jax.ShapeDtypeStruct(x.shape, x.dtype),
        scratch_shapes=[
            pltpu.VMEM((ts + 2 * HALO, d), jnp.bfloat16),
            pltpu.VMEM((ts + 2 * HALO, 4 * aw), jnp.float32),
            pltpu.VMEM((ts + 2 * HALO, aw), jnp.float32),
            pltpu.VMEM((ts, 2 * aw), jnp.bfloat16),
        ],
        compiler_params=pltpu.CompilerParams(
            dimension_semantics=("arbitrary", "arbitrary"), vmem_limit_bytes=VMEM_LIMIT_BYTES),
        name="mixer",
    )(x, x, x, *params)


def _ffn_call(h, pre_g, w_gu, w_down, post_g):
    bsz, seq, d = h.shape
    ts = SEQ_TILE
    assert seq % ts == 0 and w_down.shape[0] % FF_CHUNK == 0
    tile = pl.BlockSpec((None, ts, d), lambda b, s: (b, s, 0))
    params = (pre_g, w_gu, w_down, post_g)
    return pl.pallas_call(
        _ffn_kernel,
        grid=(bsz, seq // ts),
        in_specs=[tile] + [_resident(p.shape) for p in params],
        out_specs=tile,
        out_shape=jax.ShapeDtypeStruct(h.shape, h.dtype),
        scratch_shapes=[
            pltpu.VMEM((ts, d), jnp.bfloat16),
            pltpu.VMEM((ts, d), jnp.float32),
        ],
        compiler_params=pltpu.CompilerParams(
            dimension_semantics=("arbitrary", "arbitrary"), vmem_limit_bytes=VMEM_LIMIT_BYTES),
        name="ffn",
    )(h, *params)


def kernel(x_prompt, x_sample, mix_pre_g, w_in, a_ln_g, a_ln_b, a_sp_w, a_sp_b, b_conv_w, b_conv_b,
           b_ln_g, b_ln_b, grp_g, w_out, mix_post_g, ffn_pre_g, w_gate_up, w_down, ffn_post_g):
    depth = w_in.shape[0]
    bf16 = jnp.bfloat16
    y_prompt, y_sample = x_prompt, x_sample
    for l in range(depth):
        row = lambda p: p[l][None, :]
        sp_b = jnp.repeat(jnp.transpose(a_sp_b[l]), GROUP, axis=1)
        mixer_params = (row(mix_pre_g), w_in[l].astype(bf16), row(a_ln_g), row(a_ln_b),
                        a_sp_w[l].astype(bf16), sp_b, b_conv_w[l], row(b_conv_b), row(b_ln_g),
                        row(b_ln_b), row(grp_g), w_out[l].astype(bf16), row(mix_post_g))
        ffn_params = (row(ffn_pre_g), w_gate_up[l].astype(bf16), w_down[l].astype(bf16),
                      row(ffn_post_g))
        y_prompt = _ffn_call(_mixer_call(y_prompt, *mixer_params), *ffn_params)
        y_sample = _ffn_call(_mixer_call(y_sample, *mixer_params), *ffn_params)
    return (y_prompt, y_sample)
```

```python
import functools

import jax
import jax.numpy as jnp
from jax import lax
from jax.experimental import pallas as pl
from jax.experimental.pallas import tpu as pltpu

CHUNK = 128
HEADS = 4
GROUP = 128
CONV_WIDTH = 31
CONV_PAD = CONV_WIDTH // 2
HALO = 16
EPS = 1e-6

SEQ_TILE = 512
FF_CHUNK = 256
VMEM_LIMIT_BYTES = 56 * 1024 * 1024


def _rms(x, g):
    return x * lax.rsqrt(jnp.mean(x * x, axis=-1, keepdims=True) + EPS) * g


def _layer_norm(x, g, b):
    mu = jnp.mean(x, axis=-1, keepdims=True)
    xc = x - mu
    var = jnp.mean(xc * xc, axis=-1, keepdims=True)
    return xc * lax.rsqrt(var + EPS) * g + b


def _mixer_kernel(x_ref, xprev_ref, xnext_ref, pre_g_ref, w_in_ref, a_ln_g_ref, a_ln_b_ref,
                  sp_w_ref, sp_b_ref, conv_w_ref, conv_b_ref, b_ln_g_ref, b_ln_b_ref,
                  grp_g_ref, w_out_ref, post_g_ref, h_ref,
                  hn_ref, proj_ref, z_ref, mixed_ref):
    ts = x_ref.shape[0]
    aw = HEADS * GROUP
    s = pl.program_id(1)
    last = pl.num_programs(1) - 1
    pre_g = pre_g_ref[...]

    hn_ref[pl.ds(0, HALO), :] = _rms(xprev_ref[...], pre_g).astype(jnp.bfloat16)
    hn_ref[pl.ds(HALO, ts), :] = _rms(x_ref[...], pre_g).astype(jnp.bfloat16)
    hn_ref[pl.ds(HALO + ts, HALO), :] = _rms(xnext_ref[...], pre_g).astype(jnp.bfloat16)

    proj_ref[...] = jnp.dot(hn_ref[...], w_in_ref[...], preferred_element_type=jnp.float32)

    for h in range(HEADS):
        z_ref[h] = (proj_ref[:, pl.ds(2 * aw + h * GROUP, GROUP)]
                    * jax.nn.sigmoid(proj_ref[:, pl.ds(3 * aw + h * GROUP, GROUP)]))

    @pl.when(s == 0)
    def _():
        z_ref[:, pl.ds(0, HALO), :] = jnp.zeros((HEADS, HALO, GROUP), jnp.float32)

    @pl.when(s == last)
    def _():
        z_ref[:, pl.ds(HALO + ts, HALO), :] = jnp.zeros((HEADS, HALO, GROUP), jnp.float32)

    grp_g = grp_g_ref[...]

    def chunk_body(c, carry):
        r0 = c * CHUNK
        rows = pl.ds(pl.multiple_of(HALO + r0, HALO), CHUNK)
        out_rows = pl.ds(pl.multiple_of(r0, CHUNK), CHUNK)

        u = jax.nn.gelu(proj_ref[rows, pl.ds(0, aw)])
        v = jax.nn.gelu(proj_ref[rows, pl.ds(aw, aw)])
        out_a = []
        for h in range(HEADS):
            cols = slice(h * GROUP, (h + 1) * GROUP)
            vn = _layer_norm(v[:, cols], a_ln_g_ref[:, cols], a_ln_b_ref[:, cols])
            sv = jnp.dot(sp_w_ref[h], vn.astype(jnp.bfloat16), preferred_element_type=jnp.float32)
            out_a.append(u[:, cols] * (sv + sp_b_ref[:, cols]))
        out_a = jnp.concatenate(out_a, axis=-1)
        mixed_ref[out_rows, pl.ds(0, aw)] = _rms(out_a, grp_g[:, :aw]).astype(jnp.bfloat16)

        out_b = []
        for h in range(HEADS):
            cols = pl.ds(h * GROUP, GROUP)
            acc = jnp.zeros((CHUNK, GROUP), jnp.float32)
            for k in range(CONV_WIDTH):
                acc = acc + z_ref[h, pl.ds(r0 + HALO - CONV_PAD + k, CHUNK), :] * conv_w_ref[pl.ds(k, 1), cols]
            acc = acc + conv_b_ref[:, cols]
            out_b.append(jax.nn.silu(_layer_norm(acc, b_ln_g_ref[:, cols], b_ln_b_ref[:, cols])))
        out_b = jnp.concatenate(out_b, axis=-1)
        mixed_ref[out_rows, pl.ds(aw, aw)] = _rms(out_b, grp_g[:, aw:]).astype(jnp.bfloat16)
        return carry

    lax.fori_loop(0, ts // CHUNK, chunk_body, 0)

    mix_out = jnp.dot(mixed_ref[...], w_out_ref[...], preferred_element_type=jnp.float32)
    h_ref[...] = x_ref[...] + _rms(mix_out, post_g_ref[...])


def _ffn_kernel(h_ref, pre_g_ref, w_gu_ref, w_down_ref, post_g_ref, y_ref, hn_ref, acc_ref):
    d_ff = w_down_ref.shape[0]
    hn_ref[...] = _rms(h_ref[...], pre_g_ref[...]).astype(jnp.bfloat16)

    for j in range(d_ff // FF_CHUNK):
        c0 = j * FF_CHUNK
        hn = hn_ref[...]
        gate = jnp.dot(hn, w_gu_ref[:, pl.ds(c0, FF_CHUNK)], preferred_element_type=jnp.float32)
        up = jnp.dot(hn, w_gu_ref[:, pl.ds(d_ff + c0, FF_CHUNK)], preferred_element_type=jnp.float32)
        act = (jax.nn.silu(gate) * up).astype(jnp.bfloat16)
        down = jnp.dot(act, w_down_ref[pl.ds(c0, FF_CHUNK), :], preferred_element_type=jnp.float32)
        if j == 0:
            acc_ref[...] = down
        else:
            acc_ref[...] += down
    y_ref[...] = h_ref[...] + _rms(acc_ref[...], post_g_ref[...])


def _resident(shape):
    return pl.BlockSpec(shape, lambda b, s: (0,) * len(shape), pipeline_mode=pl.Buffered(1))


def _mixer_call(x, pre_g, w_in, a_ln_g, a_ln_b, sp_w, sp_b, conv_w, conv_b, b_ln_g, b_ln_b,
                grp_g, w_out, post_g):
    bsz, seq, d = x.shape
    ts = SEQ_TILE
    assert seq % ts == 0 and ts % CHUNK == 0 and ts % HALO == 0
    halo_blocks_per_tile = ts // HALO
    n_halo_blocks = seq // HALO
    aw = HEADS * GROUP
    tile = pl.BlockSpec((None, ts, d), lambda b, s: (b, s, 0))
    prev_halo = pl.BlockSpec(
        (None, HALO, d), lambda b, s: (b, jnp.maximum(s * halo_blocks_per_tile - 1, 0), 0))
    next_halo = pl.BlockSpec(
        (None, HALO, d),
        lambda b, s: (b, jnp.minimum((s + 1) * halo_blocks_per_tile, n_halo_blocks - 1), 0))
    params = (pre_g, w_in, a_ln_g, a_ln_b, sp_w, sp_b, conv_w, conv_b, b_ln_g, b_ln_b,
              grp_g, w_out, post_g)
    return pl.pallas_call(
        _mixer_kernel,
        grid=(bsz, seq // ts),
        in_specs=[tile, prev_halo, next_halo] + [_resident(p.shape) for p in params],
        out_specs=tile,
        out_shape=jax.ShapeDtypeStruct(x.shape, x.dtype),
        scratch_shapes=[
            pltpu.VMEM((ts + 2 * HALO, d), jnp.bfloat16),
            pltpu.VMEM((ts + 2 * HALO, 4 * aw), jnp.float32),
            pltpu.VMEM((HEADS, ts + 2 * HALO, GROUP), jnp.float32),
            pltpu.VMEM((ts, 2 * aw), jnp.bfloat16),
        ],
        compiler_params=pltpu.CompilerParams(
            dimension_semantics=("arbitrary", "arbitrary"), vmem_limit_bytes=VMEM_LIMIT_BYTES),
        name="mixer",
    )(x, x, x, *params)


def _ffn_call(h, pre_g, w_gu, w_down, post_g):
    bsz, seq, d = h.shape
    ts = SEQ_TILE
    assert seq % ts == 0 and w_down.shape[0] % FF_CHUNK == 0
    tile = pl.BlockSpec((None, ts, d), lambda b, s: (b, s, 0))
    params = (pre_g, w_gu, w_down, post_g)
    return pl.pallas_call(
        _ffn_kernel,
        grid=(bsz, seq // ts),
        in_specs=[tile] + [_resident(p.shape) for p in params],
        out_specs=tile,
        out_shape=jax.ShapeDtypeStruct(h.shape, h.dtype),
        scratch_shapes=[
            pltpu.VMEM((ts, d), jnp.bfloat16),
            pltpu.VMEM((ts, d), jnp.float32),
        ],
        compiler_params=pltpu.CompilerParams(
            dimension_semantics=("arbitrary", "arbitrary"), vmem_limit_bytes=VMEM_LIMIT_BYTES),
        name="ffn",
    )(h, *params)


def kernel(x_prompt, x_sample, mix_pre_g, w_in, a_ln_g, a_ln_b, a_sp_w, a_sp_b, b_conv_w, b_conv_b,
           b_ln_g, b_ln_b, grp_g, w_out, mix_post_g, ffn_pre_g, w_gate_up, w_down, ffn_post_g):
    depth = w_in.shape[0]
    bf16 = jnp.bfloat16
    y_prompt, y_sample = x_prompt, x_sample
    for l in range(depth):
        row = lambda p: p[l][None, :]
        sp_b = jnp.repeat(jnp.transpose(a_sp_b[l]), GROUP, axis=1)
        mixer_params = (row(mix_pre_g), w_in[l].astype(bf16), row(a_ln_g), row(a_ln_b),
                        a_sp_w[l].astype(bf16), sp_b, b_conv_w[l], row(b_conv_b), row(b_ln_g),
                        row(b_ln_b), row(grp_g), w_out[l].astype(bf16), row(mix_post_g))
        ffn_params = (row(ffn_pre_g), w_gate_up[l].astype(bf16), w_down[l].astype(bf16),
                      row(ffn_post_g))
        y_prompt = _ffn_call(_mixer_call(y_prompt, *mixer_params), *ffn_params)
        y_sample = _ffn_call(_mixer_call(y_sample, *mixer_params), *ffn_params)
    return (y_prompt, y_sample)
```

```python
import jax
import jax.numpy as jnp
from jax import lax
from jax.experimental import pallas as pl
from jax.experimental.pallas import tpu as pltpu

CHUNK = 128
HEADS = 4
GROUP = 128
CONV_WIDTH = 31
CONV_PAD = CONV_WIDTH // 2
HALO = 16
EPS = 1e-6

SEQ_TILE = 512
FF_CHUNK = 256
VMEM_LIMIT_BYTES = 58 * 1024 * 1024


def _rms(x, g):
    return x * lax.rsqrt(jnp.mean(x * x, axis=-1, keepdims=True) + EPS) * g


def _layer_norm(x, g, b):
    mu = jnp.mean(x, axis=-1, keepdims=True)
    xc = x - mu
    var = jnp.mean(xc * xc, axis=-1, keepdims=True)
    return xc * lax.rsqrt(var + EPS) * g + b


def _mixer_stages(row0, first_in_seq, last_in_seq, x_ref, xprev_ref, xnext_ref, pre_g_ref, w_in_ref,
                  a_ln_g_ref, a_ln_b_ref, sp_w_ref, sp_b_ref, conv_w_ref, conv_b_ref, b_ln_g_ref,
                  b_ln_b_ref, grp_g_ref, w_out_ref, post_g_ref, h_ref, hn_ref, proj_ref, z_ref,
                  mixed_ref):
    ts = x_ref.shape[0]
    aw = HEADS * GROUP
    pre_g = pre_g_ref[...]

    hn_ref[pl.ds(0, HALO), :] = _rms(xprev_ref[...], pre_g).astype(jnp.bfloat16)
    hn_ref[pl.ds(HALO, ts), :] = _rms(x_ref[...], pre_g).astype(jnp.bfloat16)
    hn_ref[pl.ds(HALO + ts, HALO), :] = _rms(xnext_ref[...], pre_g).astype(jnp.bfloat16)

    proj_ref[...] = jnp.dot(hn_ref[...], w_in_ref[...], preferred_element_type=jnp.float32)

    for h in range(HEADS):
        val_cols = pl.ds(2 * aw + h * GROUP, GROUP)
        gate_cols = pl.ds(3 * aw + h * GROUP, GROUP)
        for rows, is_pad in ((pl.ds(0, HALO), first_in_seq), (pl.ds(HALO, ts), None),
                             (pl.ds(HALO + ts, HALO), last_in_seq)):
            z = proj_ref[rows, val_cols] * jax.nn.sigmoid(proj_ref[rows, gate_cols])
            z_ref[h, rows, :] = z if is_pad is None else jnp.where(is_pad, 0.0, z)
    yield

    grp_g = grp_g_ref[...]
    for c in range(ts // CHUNK):
        r0 = c * CHUNK
        rows = pl.ds(HALO + r0, CHUNK)

        u = jax.nn.gelu(proj_ref[rows, pl.ds(0, aw)])
        v = jax.nn.gelu(proj_ref[rows, pl.ds(aw, aw)])
        out_a = []
        for h in range(HEADS):
            cols = slice(h * GROUP, (h + 1) * GROUP)
            vn = _layer_norm(v[:, cols], a_ln_g_ref[:, cols], a_ln_b_ref[:, cols])
            sv = jnp.dot(sp_w_ref[h], vn.astype(jnp.bfloat16), preferred_element_type=jnp.float32)
            out_a.append(u[:, cols] * (sv + sp_b_ref[:, cols]))
        out_a = jnp.concatenate(out_a, axis=-1)
        mixed_ref[pl.ds(r0, CHUNK), pl.ds(0, aw)] = _rms(out_a, grp_g[:, :aw]).astype(jnp.bfloat16)
        yield

        out_b = []
        for h in range(HEADS):
            cols = pl.ds(h * GROUP, GROUP)
            acc = jnp.zeros((CHUNK, GROUP), jnp.float32)
            for k in range(CONV_WIDTH):
                win = z_ref[h, pl.ds(row0 + (r0 + HALO - CONV_PAD + k), CHUNK), :]
                acc = acc + win * conv_w_ref[pl.ds(k, 1), cols]
            acc = acc + conv_b_ref[:, cols]
            out_b.append(jax.nn.silu(_layer_norm(acc, b_ln_g_ref[:, cols], b_ln_b_ref[:, cols])))
        out_b = jnp.concatenate(out_b, axis=-1)
        mixed_ref[pl.ds(r0, CHUNK), pl.ds(aw, aw)] = _rms(out_b, grp_g[:, aw:]).astype(jnp.bfloat16)
        yield

    mix_out = jnp.dot(mixed_ref[...], w_out_ref[...], preferred_element_type=jnp.float32)
    h_ref[...] = x_ref[...] + _rms(mix_out, post_g_ref[...])
    yield


def _ffn_stages(h_ref, pre_g_ref, w_gu_ref, w_down_ref, post_g_ref, y_ref, hn_ref, acc_ref):
    d_ff = w_down_ref.shape[0]
    hn_ref[...] = _rms(h_ref[...], pre_g_ref[...]).astype(jnp.bfloat16)
    yield
    for j in range(d_ff // FF_CHUNK):
        c0 = j * FF_CHUNK
        hn = hn_ref[...]
        gate = jnp.dot(hn, w_gu_ref[:, pl.ds(c0, FF_CHUNK)], preferred_element_type=jnp.float32)
        up = jnp.dot(hn, w_gu_ref[:, pl.ds(d_ff + c0, FF_CHUNK)], preferred_element_type=jnp.float32)
        act = (jax.nn.silu(gate) * up).astype(jnp.bfloat16)
        down = jnp.dot(act, w_down_ref[pl.ds(c0, FF_CHUNK), :], preferred_element_type=jnp.float32)
        if j == 0:
            acc_ref[...] = down
        else:
            acc_ref[...] += down
        yield
    y_ref[...] = h_ref[...] + _rms(acc_ref[...], post_g_ref[...])
    yield


def _alternate(*stage_generators):
    live = list(stage_generators)
    done = object()
    while live:
        live = [g for g in live if next(g, done) is not done]


def _layer_kernel(tiles_per_seq, zero_ref, x_ref, xprev_ref, xnext_ref,
                  pre_g_ref, w_in_ref, a_ln_g_ref, a_ln_b_ref, sp_w_ref, sp_b_ref, conv_w_ref,
                  conv_b_ref, b_ln_g_ref, b_ln_b_ref, grp_g_ref, w_out_ref, post_g_ref,
                  ffn_pre_g_ref, w_gu_ref, w_down_ref, ffn_post_g_ref,
                  y_ref,
                  h_ref, hprev_ref, hn_ref, proj_ref, z_ref, mixed_ref, hn2_ref, acc_ref):
    i = pl.program_id(0)
    n_tiles = pl.num_programs(0) - 1
    s = jnp.minimum(i, n_tiles - 1) % tiles_per_seq

    @pl.when(i == 0)
    def _():
        h_ref[...] = jnp.zeros_like(h_ref)

    hprev_ref[...] = h_ref[...]
    ffn = _ffn_stages(hprev_ref, ffn_pre_g_ref, w_gu_ref, w_down_ref, ffn_post_g_ref, y_ref,
                      hn2_ref, acc_ref)
    mixer = _mixer_stages(zero_ref[0], s == 0, s == tiles_per_seq - 1, x_ref, xprev_ref, xnext_ref,
                          pre_g_ref, w_in_ref, a_ln_g_ref, a_ln_b_ref, sp_w_ref, sp_b_ref,
                          conv_w_ref, conv_b_ref, b_ln_g_ref, b_ln_b_ref, grp_g_ref, w_out_ref,
                          post_g_ref, h_ref, hn_ref, proj_ref, z_ref, mixed_ref)
    _alternate(ffn, mixer)


def _resident(shape):
    return pl.BlockSpec(shape, lambda i: (0,) * len(shape), pipeline_mode=pl.Buffered(1))


def _layer_call(x, mixer_params, ffn_params):
    bsz, seq, d = x.shape
    ts = SEQ_TILE
    assert seq % ts == 0 and ts % CHUNK == 0 and ts % HALO == 0
    assert ffn_params[2].shape[0] % FF_CHUNK == 0
    tiles_per_seq = seq // ts
    n_tiles = bsz * tiles_per_seq
    halo_blocks_per_tile = ts // HALO
    n_halo_blocks = seq // HALO
    aw = HEADS * GROUP

    def mixer_tile(i):
        t = jnp.minimum(i, n_tiles - 1)
        return t // tiles_per_seq, t % tiles_per_seq

    def tile_index(i):
        b, s = mixer_tile(i)
        return b, s, 0

    def prev_halo_index(i):
        b, s = mixer_tile(i)
        return b, jnp.maximum(s * halo_blocks_per_tile - 1, 0), 0

    def next_halo_index(i):
        b, s = mixer_tile(i)
        return b, jnp.minimum((s + 1) * halo_blocks_per_tile, n_halo_blocks - 1), 0

    def out_index(i):
        t = jnp.maximum(i - 1, 0)
        return t // tiles_per_seq, t % tiles_per_seq, 0

    params = tuple(mixer_params) + tuple(ffn_params)
    zero = jnp.zeros((1,), jnp.int32)
    return pl.pallas_call(
        lambda *refs: _layer_kernel(tiles_per_seq, *refs),
        grid=(n_tiles + 1,),
        in_specs=[pl.BlockSpec(memory_space=pltpu.SMEM),
                  pl.BlockSpec((None, ts, d), tile_index),
                  pl.BlockSpec((None, HALO, d), prev_halo_index),
                  pl.BlockSpec((None, HALO, d), next_halo_index)]
                 + [_resident(p.shape) for p in params],
        out_specs=pl.BlockSpec((None, ts, d), out_index),
        out_shape=jax.ShapeDtypeStruct(x.shape, x.dtype),
        scratch_shapes=[
            pltpu.VMEM((ts, d), jnp.float32),
            pltpu.VMEM((ts, d), jnp.float32),
            pltpu.VMEM((ts + 2 * HALO, d), jnp.bfloat16),
            pltpu.VMEM((ts + 2 * HALO, 4 * aw), jnp.float32),
            pltpu.VMEM((HEADS, ts + 2 * HALO, GROUP), jnp.float32),
            pltpu.VMEM((ts, 2 * aw), jnp.bfloat16),
            pltpu.VMEM((ts, d), jnp.bfloat16),
            pltpu.VMEM((ts, d), jnp.float32),
        ],
        compiler_params=pltpu.CompilerParams(
            dimension_semantics=("arbitrary",), vmem_limit_bytes=VMEM_LIMIT_BYTES),
        name="encoder_layer",
    )(zero, x, x, x, *params)


def kernel(x_prompt, x_sample, mix_pre_g, w_in, a_ln_g, a_ln_b, a_sp_w, a_sp_b, b_conv_w, b_conv_b,
           b_ln_g, b_ln_b, grp_g, w_out, mix_post_g, ffn_pre_g, w_gate_up, w_down, ffn_post_g):
    depth = w_in.shape[0]
    bf16 = jnp.bfloat16
    y_prompt, y_sample = x_prompt, x_sample
    for l in range(depth):
        row = lambda p: p[l][None, :]
        sp_b = jnp.repeat(jnp.transpose(a_sp_b[l]), GROUP, axis=1)
        mixer_params = (row(mix_pre_g), w_in[l].astype(bf16), row(a_ln_g), row(a_ln_b),
                        a_sp_w[l].astype(bf16), sp_b, b_conv_w[l], row(b_conv_b), row(b_ln_g),
                        row(b_ln_b), row(grp_g), w_out[l].astype(bf16), row(mix_post_g))
        ffn_params = (row(ffn_pre_g), w_gate_up[l].astype(bf16), w_down[l].astype(bf16),
                      row(ffn_post_g))
        y_prompt = _layer_call(y_prompt, mixer_params, ffn_params)
        y_sample = _layer_call(y_sample, mixer_params, ffn_params)
    return (y_prompt, y_sample)
```
